```python
import math
import jax, jax.numpy as jnp
from jax import lax
import numpy as np

D_MODEL = 1024
BATCH = 4
SEQ = 4096
DEPTH = 1

N_META = 16
CONV_WIDTH = D_MODEL
CONV_KERNEL = 31
POOL_WIDTH = D_MODEL
POOL_WINDOWS = (2, 4, 8, 16)
N_POOL_GROUPS = len(POOL_WINDOWS)
POOL_GROUP_DIM = POOL_WIDTH // N_POOL_GROUPS
N_BRANCHES = 2
D_FF = int(math.ceil(8 * D_MODEL / 3 / 256) * 256)
D_IN = 2 * CONV_WIDTH + POOL_WIDTH + N_BRANCHES * D_MODEL
RMS_EPS = 1e-6
LN_EPS = 1e-5

kernel_name = "hybrid_conv_pool_gated_encoder_block"


def rms_norm(x, g):
    xf = x.astype(jnp.float32)
    y = xf * lax.rsqrt(jnp.mean(xf * xf, axis=-1, keepdims=True) + RMS_EPS)
    return (y * g.astype(jnp.float32)).astype(x.dtype)


def layer_norm(x, g, b):
    xf = x.astype(jnp.float32)
    mu = jnp.mean(xf, axis=-1, keepdims=True)
    xc = xf - mu
    var = jnp.mean(xc * xc, axis=-1, keepdims=True)
    y = xc * lax.rsqrt(var + LN_EPS)
    return (y * g.astype(jnp.float32) + b.astype(jnp.float32)).astype(x.dtype)


def conformer_conv(a_val, a_gate, w_dw, b_dw, ln_g, ln_b, w_conv_out):
    a = a_val * jax.nn.sigmoid(a_gate)
    pad = CONV_KERNEL // 2
    a = lax.conv_general_dilated(
        a, w_dw[:, None, :].astype(a.dtype),
        window_strides=(1,), padding=[(pad, pad)],
        dimension_numbers=("NWC", "WIO", "NWC"),
        feature_group_count=a.shape[-1]) + b_dw
    a = jax.nn.silu(layer_norm(a, ln_g, ln_b))
    return a @ w_conv_out


def multi_scale_pool(p, w_pool, pool_scale, w_pool_out):
    B, T, C = p.shape
    pf = p.astype(jnp.float32)
    cs = jnp.concatenate([jnp.zeros((B, 1, C), jnp.float32), jnp.cumsum(pf, axis=1)], axis=1)
    t = jnp.arange(T)
    outs = []
    for k, w in enumerate(POOL_WINDOWS):
        left = w // 2
        right = w - 1 - left
        lo = jnp.clip(t - left, 0, T)
        hi = jnp.clip(t + right + 1, 0, T)
        sl = slice(k * POOL_GROUP_DIM, (k + 1) * POOL_GROUP_DIM)
        csg = cs[..., sl]
        s = jnp.take(csg, hi, axis=1) - jnp.take(csg, lo, axis=1)
        cnt = (hi - lo).astype(jnp.float32)[None, :, None]
        outs.append(s / cnt - pf[..., sl])
    m = jnp.stack(outs, axis=2).astype(p.dtype)
    m = jnp.einsum("btgc,gcd->btgd", m, w_pool).reshape(B, T, C)
    return (m * pool_scale) @ w_pool_out


def setup_inputs(seed: int = 0) -> dict:
    key = jax.random.key(seed)
    ks = jax.random.split(key, 24)
    f32 = jnp.float32
    L, D = DEPTH, D_MODEL

    def nrm(k, shape, fan_in):
        return jax.random.normal(k, shape, f32) * (fan_in ** -0.5)

    def gain(k, shape):
        return 1.0 + 0.05 * jax.random.normal(k, shape, f32)

    return {
        "x": jax.random.normal(ks[0], (BATCH, SEQ, D), f32),
        "meta_tokens": jax.random.normal(ks[1], (N_META, D), f32),
        "g_mix": gain(ks[2], (L, D)),
        "w_in": nrm(ks[3], (L, D, D_IN), D),
        "b_gate": 0.02 * jax.random.normal(ks[4], (L, N_BRANCHES * D), f32),
        "w_dw": nrm(ks[5], (L, CONV_KERNEL, CONV_WIDTH), CONV_KERNEL),
        "b_dw": 0.02 * jax.random.normal(ks[6], (L, CONV_WIDTH), f32),
        "ln_g": gain(ks[7], (L, CONV_WIDTH)),
        "ln_b": 0.02 * jax.random.normal(ks[8], (L, CONV_WIDTH), f32),
        "w_conv_out": nrm(ks[9], (L, CONV_WIDTH, D), CONV_WIDTH),
        "w_pool": nrm(ks[10], (L, N_POOL_GROUPS, POOL_GROUP_DIM, POOL_GROUP_DIM), POOL_GROUP_DIM),
        "pool_scale": gain(ks[11], (L, POOL_WIDTH)),
        "w_pool_out": nrm(ks[12], (L, POOL_WIDTH, D), POOL_WIDTH),
        "w_o": nrm(ks[13], (L, D, D), D),
        "g_ffn": gain(ks[14], (L, D)),
        "w_ffn_gate": nrm(ks[15], (L, D, D_FF), D),
        "w_ffn_up": nrm(ks[16], (L, D, D_FF), D),
        "w_ffn_down": nrm(ks[17], (L, D_FF, D), D_FF),
        "g_final": gain(ks[18], (D,)),
    }


def reference(x, meta_tokens, g_mix, w_in, b_gate, w_dw, b_dw, ln_g, ln_b, w_conv_out,
              w_pool, pool_scale, w_pool_out, w_o, g_ffn, w_ffn_gate, w_ffn_up,
              w_ffn_down, g_final):
    B = x.shape[0]
    meta = jnp.broadcast_to(meta_tokens[None].astype(x.dtype), (B, N_META, D_MODEL))
    h = jnp.concatenate([meta, x], axis=1)
    c1 = CONV_WIDTH
    c2 = 2 * CONV_WIDTH
    c3 = c2 + POOL_WIDTH
    c4 = c3 + D_MODEL
    for l in range(DEPTH):
        u = rms_norm(h, g_mix[l])
        z = u @ w_in[l]
        gates = jax.nn.sigmoid(z[..., c3:] + b_gate[l])
        y_conv = conformer_conv(z[..., :c1], z[..., c1:c2], w_dw[l], b_dw[l],
                                ln_g[l], ln_b[l], w_conv_out[l])
        y_pool = multi_scale_pool(z[..., c2:c3], w_pool[l], pool_scale[l], w_pool_out[l])
        merged = gates[..., :D_MODEL] * y_conv + gates[..., D_MODEL:] * y_pool
        h = h + merged @ w_o[l]
        v = rms_norm(h, g_ffn[l])
        f = jax.nn.silu(v @ w_ffn_gate[l]) * (v @ w_ffn_up[l])
        h = h + f @ w_ffn_down[l]
    h = rms_norm(h, g_final)
    return h[:, N_META:, :]
```

```python
import functools

import jax
import jax.numpy as jnp
from jax import lax
from jax.experimental import pallas as pl
from jax.experimental.pallas import tpu as pltpu

D_MODEL = 1024
N_META = 16
CONV_KERNEL = 31
POOL_WINDOWS = (2, 4, 8, 16)
POOL_GROUP_DIM = D_MODEL // len(POOL_WINDOWS)
RMS_EPS = 1e-6
LN_EPS = 1e-5

LANES = 128
SUBLANES = 8
HALO = 16
TM_MIX = 256
TM_FFN = 512
COL_BLOCK = 256
CONV_ROWS = 64
NORM_ROWS = 32
VMEM_LIMIT = 56 * 1024 * 1024


def _dot(a, b):
    return jnp.dot(a, b, preferred_element_type=jnp.float32)


def _rms_norm(x, g):
    ms = jnp.mean(x * x, axis=-1, keepdims=True)
    return x * lax.rsqrt(ms + RMS_EPS) * g


def _mixer_kernel(xp_ref, xm_ref, xn_ref, meta_ref, gmix_ref, win_ref, bgate_ref, wdw_ref, bdw_ref,
                  lng_ref, lnb_ref, wco_ref, wpool_ref, pscale_ref, wpo_ref, wo_ref, o_ref,
                  u_s, a_s, p_s, c_s, s_s, pm_s, ps_s, m_s, *, tm, n_tiles, seq_total):
    j = pl.program_id(1)
    is_first = j == 0
    is_last = j == n_tiles - 1
    rows = tm + 2 * HALO
    d = D_MODEL
    g_mix = gmix_ref[...]

    prev = jnp.where(is_first, meta_ref[...], xp_ref[...])
    u_s[0:HALO, :] = _rms_norm(prev, g_mix).astype(jnp.bfloat16)
    for rb in range(0, tm, NORM_ROWS):
        u_s[HALO + rb:HALO + rb + NORM_ROWS, :] = _rms_norm(
            xm_ref[rb:rb + NORM_ROWS, :], g_mix).astype(jnp.bfloat16)
    u_s[HALO + tm:rows, :] = _rms_norm(xn_ref[...], g_mix).astype(jnp.bfloat16)

    def store_masked(dst, cb, val):
        dst[0:HALO + tm, cb:cb + COL_BLOCK] = val[0:HALO + tm]
        dst[HALO + tm:rows, cb:cb + COL_BLOCK] = jnp.where(is_last, 0.0, val[HALO + tm:rows])

    u_all = u_s[...]
    for cb in range(0, d, COL_BLOCK):
        zv = _dot(u_all, win_ref[:, cb:cb + COL_BLOCK])
        zg = _dot(u_all, win_ref[:, d + cb:d + cb + COL_BLOCK])
        store_masked(a_s, cb, zv * jax.nn.sigmoid(zg))
    for cb in range(0, d, COL_BLOCK):
        store_masked(p_s, cb, _dot(u_all, win_ref[:, 2 * d + cb:2 * d + cb + COL_BLOCK]))

    for rb in range(0, tm, CONV_ROWS):
        for lt in range(0, d, LANES):
            blk = a_s[rb:rb + CONV_ROWS + 2 * HALO, lt:lt + LANES]
            acc = None
            for r in range(SUBLANES):
                shifted = blk[r:r + CONV_ROWS + 24]
                part = None
                for q in range(4):
                    k = SUBLANES * q + r - 1
                    if k < 0 or k >= CONV_KERNEL:
                        continue
                    term = shifted[SUBLANES * q:SUBLANES * q + CONV_ROWS] * wdw_ref[k:k + 1, lt:lt + LANES]
                    part = term if part is None else part + term
                acc = part if acc is None else acc + part
            c_s[rb:rb + CONV_ROWS, lt:lt + LANES] = acc + bdw_ref[:, lt:lt + LANES]

    ln_g = lng_ref[...]
    ln_b = lnb_ref[...]
    for rb in range(0, tm, NORM_ROWS):
        c = c_s[rb:rb + NORM_ROWS, :]
        mu = jnp.mean(c, axis=-1, keepdims=True)
        xc = c - mu
        var = jnp.mean(xc * xc, axis=-1, keepdims=True)
        y = xc * lax.rsqrt(var + LN_EPS) * ln_g + ln_b
        s_s[rb:rb + NORM_ROWS, :] = (y * jax.nn.sigmoid(y)).astype(jnp.bfloat16)

    for rb in range(0, tm, CONV_ROWS):
        row_h = lax.broadcasted_iota(jnp.int32, (CONV_ROWS, LANES), 0) + (j * tm + rb + N_META)
        for g, w in enumerate(POOL_WINDOWS):
            left = w // 2
            cnt = jnp.minimum(w, seq_total - row_h + left).astype(jnp.float32)
            inv = 1.0 / cnt
            for lt in range(g * POOL_GROUP_DIM, (g + 1) * POOL_GROUP_DIM, LANES):
                blk = p_s[rb + 8:rb + 8 + CONV_ROWS + 16, lt:lt + LANES]
                centre = blk[8:8 + CONV_ROWS]
                arr, start = blk, 0
                levels = [(-1, 0)]
                step = 1
                while 2 * step < w:
                    levels.append((-step, step))
                    step *= 2
                for dl, dr in levels:
                    n = arr.shape[0] - (dr - dl)
                    arr = arr[0:n] + arr[dr - dl:dr - dl + n]
                    start = start - dl
                win_sum = arr[8 - start:8 - start + CONV_ROWS]
                pm_s[rb:rb + CONV_ROWS, lt:lt + LANES] = (win_sum * inv - centre).astype(jnp.bfloat16)

    for g in range(len(POOL_WINDOWS)):
        sl = slice(g * POOL_GROUP_DIM, (g + 1) * POOL_GROUP_DIM)
        pw = _dot(pm_s[:, sl], wpool_ref[g])
        ps_s[:, sl] = (pw * pscale_ref[:, sl]).astype(jnp.bfloat16)

    u_main = u_s[HALO:HALO + tm, :]
    s_all = s_s[...]
    ps_all = ps_s[...]
    for cb in range(0, d, COL_BLOCK):
        cs = slice(cb, cb + COL_BLOCK)
        y_conv = _dot(s_all, wco_ref[:, cs])
        gate_a = jax.nn.sigmoid(_dot(u_main, win_ref[:, 3 * d + cb:3 * d + cb + COL_BLOCK]) + bgate_ref[:, cs])
        y_pool = _dot(ps_all, wpo_ref[:, cs])
        gate_b = jax.nn.sigmoid(
            _dot(u_main, win_ref[:, 4 * d + cb:4 * d + cb + COL_BLOCK]) + bgate_ref[:, d + cb:d + cb + COL_BLOCK])
        m_s[:, cs] = (gate_a * y_conv + gate_b * y_pool).astype(jnp.bfloat16)

    m_all = m_s[...]
    for cb in range(0, d, COL_BLOCK):
        cs = slice(cb, cb + COL_BLOCK)
        o_ref[:, cs] = xm_ref[:, cs] + _dot(m_all, wo_ref[:, cs])


def _ffn_kernel(h_ref, gffn_ref, wg_ref, wu_ref, wd_ref, gfin_ref, o_ref, v_s, f_s, h2_s, *, tm, d_ff):
    g_ffn = gffn_ref[...]
    for rb in range(0, tm, NORM_ROWS):
        v_s[rb:rb + NORM_ROWS, :] = _rms_norm(h_ref[rb:rb + NORM_ROWS, :], g_ffn).astype(jnp.bfloat16)
    v_all = v_s[...]
    for cb in range(0, d_ff, COL_BLOCK):
        cs = slice(cb, cb + COL_BLOCK)
        gate = _dot(v_all, wg_ref[:, cs])
        up = _dot(v_all, wu_ref[:, cs])
        f_s[:, cs] = (gate * jax.nn.sigmoid(gate) * up).astype(jnp.bfloat16)
    f_all = f_s[...]
    for cb in range(0, D_MODEL, COL_BLOCK):
        cs = slice(cb, cb + COL_BLOCK)
        h2_s[:, cs] = h_ref[:, cs] + _dot(f_all, wd_ref[:, cs])
    g_fin = gfin_ref[...]
    for rb in range(0, tm, NORM_ROWS):
        o_ref[rb:rb + NORM_ROWS, :] = _rms_norm(h2_s[rb:rb + NORM_ROWS, :], g_fin)


def _resident(shape):
    zeros = (0,) * len(shape)
    return pl.BlockSpec(shape, lambda b, j: zeros, pipeline_mode=pl.Buffered(1))


def kernel(x, meta_tokens, g_mix, w_in, b_gate, w_dw, b_dw, ln_g, ln_b, w_conv_out, w_pool, pool_scale,
           w_pool_out, w_o, g_ffn, w_ffn_gate, w_ffn_up, w_ffn_down, g_final):
    batch, seq, d = x.shape
    assert d == D_MODEL and meta_tokens.shape == (N_META, d) and g_mix.shape[0] == 1
    assert seq % TM_MIX == 0 and seq % TM_FFN == 0 and N_META == HALO
    d_ff = w_ffn_gate.shape[-1]
    assert d_ff % COL_BLOCK == 0
    bf16 = jnp.bfloat16
    f32 = jnp.float32

    tm = TM_MIX
    n_tiles = seq // tm
    halo_blocks = seq // HALO
    mixer = pl.pallas_call(
        functools.partial(_mixer_kernel, tm=tm, n_tiles=n_tiles, seq_total=seq + N_META),
        grid=(batch, n_tiles),
        in_specs=[
            pl.BlockSpec((None, HALO, d), lambda b, j: (b, jnp.maximum(j * (tm // HALO) - 1, 0), 0)),
            pl.BlockSpec((None, tm, d), lambda b, j: (b, j, 0)),
            pl.BlockSpec((None, HALO, d),
                         lambda b, j: (b, jnp.minimum((j + 1) * (tm // HALO), halo_blocks - 1), 0)),
            _resident((N_META, d)),
            _resident((1, d)),
            _resident((d, 5 * d)),
            _resident((1, 2 * d)),
            _resident((CONV_KERNEL, d)),
            _resident((1, d)),
            _resident((1, d)),
            _resident((1, d)),
            _resident((d, d)),
            _resident((len(POOL_WINDOWS), POOL_GROUP_DIM, POOL_GROUP_DIM)),
            _resident((1, d)),
            _resident((d, d)),
            _resident((d, d)),
        ],
        out_specs=pl.BlockSpec((None, tm, d), lambda b, j: (b, j, 0)),
        out_shape=jax.ShapeDtypeStruct((batch, seq, d), f32),
        scratch_shapes=[
            pltpu.VMEM((tm + 2 * HALO, d), bf16),
            pltpu.VMEM((tm + 2 * HALO, d), f32),
            pltpu.VMEM((tm + 2 * HALO, d), f32),
            pltpu.VMEM((tm, d), f32),
            pltpu.VMEM((tm, d), bf16),
            pltpu.VMEM((tm, d), bf16),
            pltpu.VMEM((tm, d), bf16),
            pltpu.VMEM((tm, d), bf16),
        ],
        compiler_params=pltpu.CompilerParams(
            dimension_semantics=("arbitrary", "arbitrary"), vmem_limit_bytes=VMEM_LIMIT),
        name="mixer",
    )
    h1 = mixer(x, x, x, meta_tokens.astype(f32), g_mix.astype(f32), w_in[0].astype(bf16), b_gate.astype(f32),
               w_dw[0].astype(f32), b_dw.astype(f32), ln_g.astype(f32), ln_b.astype(f32),
               w_conv_out[0].astype(bf16), w_pool[0].astype(bf16), pool_scale.astype(f32),
               w_pool_out[0].astype(bf16), w_o[0].astype(bf16))

    tf = TM_FFN
    ffn = pl.pallas_call(
        functools.partial(_ffn_kernel, tm=tf, d_ff=d_ff),
        grid=(batch, seq // tf),
        in_specs=[
            pl.BlockSpec((None, tf, d), lambda b, j: (b, j, 0)),
            _resident((1, d)),
            _resident((d, d_ff)),
            _resident((d, d_ff)),
            _resident((d_ff, d)),
            _resident((1, d)),
        ],
        out_specs=pl.BlockSpec((None, tf, d), lambda b, j: (b, j, 0)),
        out_shape=jax.ShapeDtypeStruct((batch, seq, d), x.dtype),
        scratch_shapes=[
            pltpu.VMEM((tf, d), bf16),
            pltpu.VMEM((tf, d_ff), bf16),
            pltpu.VMEM((tf, d), f32),
        ],
        compiler_params=pltpu.CompilerParams(
            dimension_semantics=("arbitrary", "arbitrary"), vmem_limit_bytes=VMEM_LIMIT),
        name="ffn",
    )
    return ffn(h1, g_ffn.astype(f32), w_ffn_gate[0].astype(bf16), w_ffn_up[0].astype(bf16),
               w_ffn_down[0].astype(bf16), g_final.reshape(1, d).astype(f32))
```

```python
import functools

import jax
import jax.numpy as jnp
from jax import lax
from jax.experimental import pallas as pl
from jax.experimental.pallas import tpu as pltpu

D_MODEL = 1024
N_META = 16
CONV_KERNEL = 31
POOL_WINDOWS = (2, 4, 8, 16)
POOL_GROUP_DIM = D_MODEL // len(POOL_WINDOWS)
RMS_EPS = 1e-6
LN_EPS = 1e-5

LANES = 128
SUBLANES = 8
HALO = 16
TM = 256
COL_BLOCK = 256
CONV_ROWS = 64
NORM_ROWS = 32
VMEM_LIMIT = 58 * 1024 * 1024


def _dot(a, b):
    return jnp.dot(a, b, preferred_element_type=jnp.float32)


def _rms_norm(x, g):
    ms = jnp.mean(x * x, axis=-1, keepdims=True)
    return x * lax.rsqrt(ms + RMS_EPS) * g


def _order_token(x):
    bits = lax.bitcast_convert_type(x, jnp.int32)
    return lax.shift_right_logical(lax.shift_right_logical(bits, 16), 16).astype(jnp.float32)


def _gated(value, token):
    if token is None:
        return value
    return value + jnp.concatenate([token] * (value.shape[0] // SUBLANES), axis=0)


def _spread(vector_work, matmul_work):
    v_total = sum(c for c, _ in vector_work)
    m_total = sum(c for c, _ in matmul_work)
    out, vi, v_start, m_end = [], 0, 0.0, 0.0
    for cost, thunk in matmul_work:
        out.append(thunk)
        m_end += cost
        while vi < len(vector_work) and v_start * m_total < m_end * v_total:
            out.append(vector_work[vi][1])
            v_start += vector_work[vi][0]
            vi += 1
    out.extend(t for _, t in vector_work[vi:])
    return out


def _fused_kernel(xp_ref, xm_ref, xn_ref, xr_ref, meta_ref, gmix_ref, win_ref, bgate_ref, wdw_ref, bdw_ref,
                  lng_ref, lnb_ref, wco_ref, wpool_ref, pscale_ref, wpo_ref, wo_ref,
                  gffn_ref, wg_ref, wu_ref, wd_ref, gfin_ref, o_ref,
                  u_s, a_s, p_s, c_s, s_s, pm_s, ps_s, m_s, g_s, h1_s, v_s, f_s, h2_s,
                  *, tm, n_tiles, n_total, seq_total, d_ff):
    s = pl.program_id(0)
    j_a = lax.rem(jnp.minimum(s, n_total - 1), n_tiles)
    j_b = lax.rem(jnp.clip(s - 1, 0, n_total - 1), n_tiles)
    is_first = j_a == 0
    is_last = j_a == n_tiles - 1
    rows = tm + 2 * HALO
    d = D_MODEL
    P = functools.partial
    col_blocks = range(0, d, COL_BLOCK)
    released = [None]

    def release(result):
        released[0] = _order_token(result[0:SUBLANES, 0:LANES])

    @pl.when(s == 0)
    def _():
        for ref in (u_s, a_s, p_s, s_s, pm_s, g_s):
            ref[...] = jnp.zeros(ref.shape, ref.dtype)

    def conv_block(rb, lt):
        blk = a_s[rb:rb + CONV_ROWS + 2 * HALO, lt:lt + LANES]
        acc = None
        for r in range(SUBLANES):
            shifted = blk[r:r + CONV_ROWS + 24]
            part = None
            for q in range(4):
                k = SUBLANES * q + r - 1
                if k < 0 or k >= CONV_KERNEL:
                    continue
                term = shifted[SUBLANES * q:SUBLANES * q + CONV_ROWS] * wdw_ref[k:k + 1, lt:lt + LANES]
                part = term if part is None else part + term
            acc = _gated(part, released[0]) if acc is None else acc + part
        c_s[rb:rb + CONV_ROWS, lt:lt + LANES] = acc + bdw_ref[:, lt:lt + LANES]

    def ln_block(rb):
        c = c_s[rb:rb + NORM_ROWS, :]
        c = jnp.concatenate([_gated(c[:, 0:LANES], released[0]), c[:, LANES:]], axis=1)
        mu = jnp.mean(c, axis=-1, keepdims=True)
        xc = c - mu
        var = jnp.mean(xc * xc, axis=-1, keepdims=True)
        y = xc * lax.rsqrt(var + LN_EPS) * lng_ref[...] + lnb_ref[...]
        s_s[rb:rb + NORM_ROWS, :] = (y * jax.nn.sigmoid(y)).astype(jnp.bfloat16)

    def pool_block(rb, g):
        w = POOL_WINDOWS[g]
        row_h = lax.broadcasted_iota(jnp.int32, (CONV_ROWS, LANES), 0) + (j_b * tm + rb + N_META)
        cnt = jnp.minimum(w, seq_total - row_h + w // 2).astype(jnp.float32)
        inv = 1.0 / cnt
        for lt in range(g * POOL_GROUP_DIM, (g + 1) * POOL_GROUP_DIM, LANES):
            blk = _gated(p_s[rb + 8:rb + 8 + CONV_ROWS + 16, lt:lt + LANES], released[0])
            centre = blk[8:8 + CONV_ROWS]
            arr, start = blk, 0
            levels = [(-1, 0)]
            step = 1
            while 2 * step < w:
                levels.append((-step, step))
                step *= 2
            for dl, dr in levels:
                n = arr.shape[0] - (dr - dl)
                arr = arr[0:n] + arr[dr - dl:dr - dl + n]
                start = start - dl
            win_sum = arr[8 - start:8 - start + CONV_ROWS]
            pm_s[rb:rb + CONV_ROWS, lt:lt + LANES] = (win_sum * inv - centre).astype(jnp.bfloat16)

    def gate_block(cb):
        u_main = u_s[HALO:HALO + tm, :]
        for half in range(2):
            cs = slice(half * d + cb, half * d + cb + COL_BLOCK)
            z = _dot(u_main, win_ref[:, 3 * d + half * d + cb:3 * d + half * d + cb + COL_BLOCK])
            g_s[:, cs] = jax.nn.sigmoid(z + bgate_ref[:, cs])
        release(z)

    def pool_map_block(g):
        sl = slice(g * POOL_GROUP_DIM, (g + 1) * POOL_GROUP_DIM)
        pw = _dot(pm_s[:, sl], wpool_ref[g])
        release(pw)
        ps_s[:, sl] = (pw * pscale_ref[:, sl]).astype(jnp.bfloat16)

    def merge_block(cb):
        cs = slice(cb, cb + COL_BLOCK)
        y_conv = _dot(s_s[...], wco_ref[:, cs])
        y_pool = _dot(ps_s[...], wpo_ref[:, cs])
        release(y_pool)
        m_s[:, cs] = (g_s[:, cs] * y_conv + g_s[:, d + cb:d + cb + COL_BLOCK] * y_pool).astype(jnp.bfloat16)

    def out_proj_block(cb):
        cs = slice(cb, cb + COL_BLOCK)
        y = _dot(m_s[...], wo_ref[:, cs])
        release(y)
        h1_s[:, cs] = xr_ref[:, cs] + y

    def ffn_norm_block(rb):
        v_s[rb:rb + NORM_ROWS, :] = _rms_norm(h1_s[rb:rb + NORM_ROWS, :], gffn_ref[...]).astype(jnp.bfloat16)

    def ffn_hidden_block(cb):
        cs = slice(cb, cb + COL_BLOCK)
        gate = _dot(v_s[...], wg_ref[:, cs])
        up = _dot(v_s[...], wu_ref[:, cs])
        release(up)
        f_s[:, cs] = (gate * jax.nn.sigmoid(gate) * up).astype(jnp.bfloat16)

    def ffn_down_block(cb):
        cs = slice(cb, cb + COL_BLOCK)
        y = _dot(f_s[...], wd_ref[:, cs])
        release(y)
        h2_s[:, cs] = h1_s[:, cs] + y

    def ffn_out_block(rb):
        o_ref[rb:rb + NORM_ROWS, :] = _rms_norm(h2_s[rb:rb + NORM_ROWS, :], gfin_ref[...])

    def norm_in():
        g_mix = gmix_ref[...]
        prev = jnp.where(is_first, meta_ref[...], xp_ref[...])
        u_s[0:HALO, :] = _rms_norm(prev, g_mix).astype(jnp.bfloat16)
        for rb in range(0, tm, NORM_ROWS):
            u_s[HALO + rb:HALO + rb + NORM_ROWS, :] = _rms_norm(
                xm_ref[rb:rb + NORM_ROWS, :], g_mix).astype(jnp.bfloat16)
        u_s[HALO + tm:rows, :] = _rms_norm(xn_ref[...], g_mix).astype(jnp.bfloat16)

    def store_masked(dst, cb, val):
        dst[0:HALO + tm, cb:cb + COL_BLOCK] = val[0:HALO + tm]
        dst[HALO + tm:rows, cb:cb + COL_BLOCK] = jnp.where(is_last, 0.0, val[HALO + tm:rows])

    def glu_block(cb):
        zv = _dot(u_s[...], win_ref[:, cb:cb + COL_BLOCK])
        zg = _dot(u_s[...], win_ref[:, d + cb:d + cb + COL_BLOCK])
        store_masked(a_s, cb, zv * jax.nn.sigmoid(zg))

    def pool_proj_block(cb):
        store_masked(p_s, cb, _dot(u_s[...], win_ref[:, 2 * d + cb:2 * d + cb + COL_BLOCK]))

    row_blocks = range(0, tm, NORM_ROWS)
    conv_row_blocks = range(0, tm, CONV_ROWS)
    dot_cost = 256.0
    vector_work = []
    for rb in conv_row_blocks:
        vector_work += [(150.0, P(conv_block, rb, lt)) for lt in range(0, d, LANES)]
        vector_work += [(90.0, P(ln_block, r2)) for r2 in range(rb, rb + CONV_ROWS, NORM_ROWS)]
    vector_work += [(50.0, P(pool_block, rb, g)) for rb in conv_row_blocks for g in range(len(POOL_WINDOWS))]
    matmul_work = ([(dot_cost / 4, P(pool_map_block, g)) for g in range(len(POOL_WINDOWS))]
                   + [(2 * dot_cost, P(merge_block, cb)) for cb in col_blocks]
                   + [(dot_cost, P(out_proj_block, cb)) for cb in col_blocks]
                   + [(10.0, P(ffn_norm_block, rb)) for rb in row_blocks]
                   + [(2 * dot_cost, P(ffn_hidden_block, cb)) for cb in range(0, d_ff, COL_BLOCK)]
                   + [(d_ff / d * dot_cost, P(ffn_down_block, cb)) for cb in col_blocks]
                   + [(10.0, P(ffn_out_block, rb)) for rb in row_blocks]
                   + [(2 * dot_cost, P(gate_block, cb)) for cb in col_blocks])
    stage_a = ([norm_in]
               + [P(glu_block, cb) for cb in col_blocks]
               + [P(pool_proj_block, cb) for cb in col_blocks])
    order = _spread(vector_work, matmul_work)
    names = [t.func.__name__ for t in order]
    assert max(i for i, n in enumerate(names) if n == "merge_block") < names.index("ln_block")
    for thunk in order + stage_a:
        thunk()


def _resident(shape):
    zeros = (0,) * len(shape)
    return pl.BlockSpec(shape, lambda s: zeros, pipeline_mode=pl.Buffered(1))


def kernel(x, meta_tokens, g_mix, w_in, b_gate, w_dw, b_dw, ln_g, ln_b, w_conv_out, w_pool, pool_scale,
           w_pool_out, w_o, g_ffn, w_ffn_gate, w_ffn_up, w_ffn_down, g_final):
    batch, seq, d = x.shape
    assert d == D_MODEL and meta_tokens.shape == (N_META, d) and g_mix.shape[0] == 1
    assert seq % TM == 0 and N_META == HALO
    d_ff = w_ffn_gate.shape[-1]
    assert d_ff % COL_BLOCK == 0
    bf16 = jnp.bfloat16
    f32 = jnp.float32

    tm = TM
    n_tiles = seq // tm
    n_total = batch * n_tiles
    halo_per_tile = tm // HALO
    halo_blocks = seq // HALO

    def new_tile(s):
        sm = jnp.minimum(s, n_total - 1)
        return sm // n_tiles, sm % n_tiles

    def xp_map(s):
        b, j = new_tile(s)
        return b, jnp.maximum(j * halo_per_tile - 1, 0), 0

    def xm_map(s):
        b, j = new_tile(s)
        return b, j, 0

    def xn_map(s):
        b, j = new_tile(s)
        return b, jnp.minimum((j + 1) * halo_per_tile, halo_blocks - 1), 0

    def old_map(s):
        so = jnp.clip(s - 2, 0, n_total - 1)
        return so // n_tiles, so % n_tiles, 0

    fused = pl.pallas_call(
        functools.partial(_fused_kernel, tm=tm, n_tiles=n_tiles, n_total=n_total,
                          seq_total=seq + N_META, d_ff=d_ff),
        grid=(n_total + 2,),
        in_specs=[
            pl.BlockSpec((None, HALO, d), xp_map),
            pl.BlockSpec((None, tm, d), xm_map),
            pl.BlockSpec((None, HALO, d), xn_map),
            pl.BlockSpec((None, tm, d), old_map),
            _resident((N_META, d)),
            _resident((1, d)),
            _resident((d, 5 * d)),
            _resident((1, 2 * d)),
            _resident((CONV_KERNEL, d)),
            _resident((1, d)),
            _resident((1, d)),
            _resident((1, d)),
            _resident((d, d)),
            _resident((len(POOL_WINDOWS), POOL_GROUP_DIM, POOL_GROUP_DIM)),
            _resident((1, d)),
            _resident((d, d)),
            _resident((d, d)),
            _resident((1, d)),
            _resident((d, d_ff)),
            _resident((d, d_ff)),
            _resident((d_ff, d)),
            _resident((1, d)),
        ],
        out_specs=pl.BlockSpec((None, tm, d), old_map),
        out_shape=jax.ShapeDtypeStruct((batch, seq, d), x.dtype),
        scratch_shapes=[
            pltpu.VMEM((tm + 2 * HALO, d), bf16),
            pltpu.VMEM((tm + 2 * HALO, d), f32),
            pltpu.VMEM((tm + 2 * HALO, d), f32),
            pltpu.VMEM((tm, d), f32),
            pltpu.VMEM((tm, d), bf16),
            pltpu.VMEM((tm, d), bf16),
            pltpu.VMEM((tm, d), bf16),
            pltpu.VMEM((tm, d), bf16),
            pltpu.VMEM((tm, 2 * d), f32),
            pltpu.VMEM((tm, d), f32),
            pltpu.VMEM((tm, d), bf16),
            pltpu.VMEM((tm, d_ff), bf16),
            pltpu.VMEM((tm, d), f32),
        ],
        compiler_params=pltpu.CompilerParams(
            dimension_semantics=("arbitrary",), vmem_limit_bytes=VMEM_LIMIT),
        name="encoder_block",
    )
    return fused(x, x, x, x, meta_tokens.astype(f32), g_mix.astype(f32), w_in[0].astype(bf16), b_gate.astype(f32),
                 w_dw[0].astype(f32), b_dw.astype(f32), ln_g.astype(f32), ln_b.astype(f32),
                 w_conv_out[0].astype(bf16), w_pool[0].astype(bf16), pool_scale.astype(f32),
                 w_pool_out[0].astype(bf16), w_o[0].astype(bf16),
                 g_ffn.astype(f32), w_ffn_gate[0].astype(bf16), w_ffn_up[0].astype(bf16),
                 w_ffn_down[0].astype(bf16), g_final.reshape(1, d).astype(f32))
```

```python
import functools

import jax
import jax.numpy as jnp
from jax import lax
from jax.experimental import pallas as pl
from jax.experimental.pallas import tpu as pltpu

D_MODEL = 1024
N_META = 16
CONV_KERNEL = 31
POOL_WINDOWS = (2, 4, 8, 16)
POOL_GROUP_DIM = D_MODEL // len(POOL_WINDOWS)
RMS_EPS = 1e-6
LN_EPS = 1e-5

LANES = 128
SUBLANES = 8
HALO = 16
TM = 256
COL_BLOCK = 512
CONV_ROWS = 64
NORM_ROWS = 32
VMEM_LIMIT = 58 * 1024 * 1024


def _dot(a, b):
    return jnp.dot(a, b, preferred_element_type=jnp.float32)


def _rms_norm(x, g):
    ms = jnp.mean(x * x, axis=-1, keepdims=True)
    return x * lax.rsqrt(ms + RMS_EPS) * g


def _order_token(x):
    bits = lax.bitcast_convert_type(x, jnp.int32)
    return lax.shift_right_logical(lax.shift_right_logical(bits, 16), 16).astype(jnp.float32)


def _gated(value, token):
    if token is None:
        return value
    return value + jnp.concatenate([token] * (value.shape[0] // SUBLANES), axis=0)


def _spread(vector_work, matmul_work):
    v_total = sum(c for c, _ in vector_work)
    m_total = sum(c for c, _ in matmul_work)
    out, vi, v_start, m_end = [], 0, 0.0, 0.0
    for cost, thunk in matmul_work:
        out.append(thunk)
        m_end += cost
        while vi < len(vector_work) and v_start * m_total < m_end * v_total:
            out.append(vector_work[vi][1])
            v_start += vector_work[vi][0]
            vi += 1
    out.extend(t for _, t in vector_work[vi:])
    return out


def _fused_kernel(xp_ref, xm_ref, xn_ref, xr_ref, meta_ref, gmix_ref, win_ref, bgate_ref, wdw_ref, bdw_ref,
                  lng_ref, lnb_ref, wco_ref, wpool_ref, pscale_ref, wpo_ref, wo_ref,
                  gffn_ref, wg_ref, wu_ref, wd_ref, gfin_ref, o_ref,
                  u_s, a_s, p_s, c_s, s_s, pm_s, ps_s, m_s, g_s, h1_s, v_s, f_s, h2_s,
                  *, tm, n_tiles, n_total, seq_total, d_ff):
    s = pl.program_id(0)
    j_a = lax.rem(jnp.minimum(s, n_total - 1), n_tiles)
    j_b = lax.rem(jnp.clip(s - 1, 0, n_total - 1), n_tiles)
    is_first = j_a == 0
    is_last = j_a == n_tiles - 1
    rows = tm + 2 * HALO
    d = D_MODEL
    P = functools.partial
    col_blocks = range(0, d, COL_BLOCK)
    fresh, last = [], [None]

    def release(*results):
        fresh[:] = [_order_token(r[r0:r0 + SUBLANES, 0:LANES]) for r in results for r0 in (0, tm // 2)]

    def token():
        if fresh:
            last[0] = fresh.pop(0)
        return last[0]

    @pl.when(s == 0)
    def _():
        for ref in (u_s, a_s, p_s, s_s, pm_s, g_s):
            ref[...] = jnp.zeros(ref.shape, ref.dtype)

    def conv_block(rb, lt):
        tok = token()
        acc = None
        for k in range(CONV_KERNEL):
            window = a_s[lt // LANES, pl.ds(2 * (rb + k + 1), CONV_ROWS, stride=2), :]
            term = window * wdw_ref[k:k + 1, lt:lt + LANES]
            acc = _gated(term, tok) if acc is None else acc + term
        c_s[rb:rb + CONV_ROWS, lt:lt + LANES] = acc + bdw_ref[:, lt:lt + LANES]

    def ln_block(rb):
        c = c_s[rb:rb + NORM_ROWS, :]
        c = jnp.concatenate([_gated(c[:, 0:LANES], token()), c[:, LANES:]], axis=1)
        mu = jnp.mean(c, axis=-1, keepdims=True)
        xc = c - mu
        var = jnp.mean(xc * xc, axis=-1, keepdims=True)
        y = xc * lax.rsqrt(var + LN_EPS) * lng_ref[...] + lnb_ref[...]
        s_s[rb:rb + NORM_ROWS, :] = (y * jax.nn.sigmoid(y)).astype(jnp.bfloat16)

    def pool_block(rb, g):
        w = POOL_WINDOWS[g]
        row_h = lax.broadcasted_iota(jnp.int32, (CONV_ROWS, LANES), 0) + (j_b * tm + rb + N_META)
        cnt = jnp.minimum(w, seq_total - row_h + w // 2).astype(jnp.float32)
        inv = 1.0 / cnt
        tok = token()
        for lt in range(g * POOL_GROUP_DIM, (g + 1) * POOL_GROUP_DIM, LANES):
            blk = _gated(p_s[rb + 8:rb + 8 + CONV_ROWS + 16, lt:lt + LANES], tok)
            centre = blk[8:8 + CONV_ROWS]
            arr, start = blk, 0
            levels = [(-1, 0)]
            step = 1
            while 2 * step < w:
                levels.append((-step, step))
                step *= 2
            for dl, dr in levels:
                n = arr.shape[0] - (dr - dl)
                arr = arr[0:n] + arr[dr - dl:dr - dl + n]
                start = start - dl
            win_sum = arr[8 - start:8 - start + CONV_ROWS]
            pm_s[rb:rb + CONV_ROWS, lt:lt + LANES] = (win_sum * inv - centre).astype(jnp.bfloat16)

    def gate_block(cb):
        u_main = u_s[HALO:HALO + tm, :]
        zs = []
        for half in range(2):
            cs = slice(half * d + cb, half * d + cb + COL_BLOCK)
            zs.append(_dot(u_main, win_ref[:, 3 * d + half * d + cb:3 * d + half * d + cb + COL_BLOCK]))
            g_s[:, cs] = jax.nn.sigmoid(zs[-1] + bgate_ref[:, cs])
        release(*zs)

    def pool_map_block(g):
        sl = slice(g * POOL_GROUP_DIM, (g + 1) * POOL_GROUP_DIM)
        pw = _dot(pm_s[:, sl], wpool_ref[g])
        release(pw)
        ps_s[:, sl] = (pw * pscale_ref[:, sl]).astype(jnp.bfloat16)

    def merge_block(cb):
        cs = slice(cb, cb + COL_BLOCK)
        y_conv = _dot(s_s[...], wco_ref[:, cs])
        y_pool = _dot(ps_s[...], wpo_ref[:, cs])
        release(y_conv, y_pool)
        m_s[:, cs] = (g_s[:, cs] * y_conv + g_s[:, d + cb:d + cb + COL_BLOCK] * y_pool).astype(jnp.bfloat16)

    def out_proj_block(cb):
        cs = slice(cb, cb + COL_BLOCK)
        y = _dot(m_s[...], wo_ref[:, cs])
        release(y)
        h1_s[:, cs] = xr_ref[:, cs] + y

    def ffn_norm_block(rb):
        v_s[rb:rb + NORM_ROWS, :] = _rms_norm(h1_s[rb:rb + NORM_ROWS, :], gffn_ref[...]).astype(jnp.bfloat16)

    def ffn_hidden_block(cb):
        cs = slice(cb, min(cb + COL_BLOCK, d_ff))
        gate = _dot(v_s[...], wg_ref[:, cs])
        up = _dot(v_s[...], wu_ref[:, cs])
        release(gate, up)
        f_s[:, cs] = (gate * jax.nn.sigmoid(gate) * up).astype(jnp.bfloat16)

    def ffn_down_block(cb):
        cs = slice(cb, cb + COL_BLOCK)
        k_half = (d_ff // 2 + 255) // 256 * 256
        y_lo = _dot(f_s[:, 0:k_half], wd_ref[0:k_half, cs])
        y_hi = _dot(f_s[:, k_half:d_ff], wd_ref[k_half:d_ff, cs])
        release(y_lo, y_hi)
        h2_s[:, cs] = h1_s[:, cs] + (y_lo + y_hi)

    def ffn_out_block(rb):
        o_ref[rb:rb + NORM_ROWS, :] = _rms_norm(h2_s[rb:rb + NORM_ROWS, :], gfin_ref[...])

    def norm_in():
        g_mix = gmix_ref[...]
        prev = jnp.where(is_first, meta_ref[...], xp_ref[...])
        u_s[0:HALO, :] = _rms_norm(prev, g_mix).astype(jnp.bfloat16)
        for rb in range(0, tm, NORM_ROWS):
            u_s[HALO + rb:HALO + rb + NORM_ROWS, :] = _rms_norm(
                xm_ref[rb:rb + NORM_ROWS, :], g_mix).astype(jnp.bfloat16)
        u_s[HALO + tm:rows, :] = _rms_norm(xn_ref[...], g_mix).astype(jnp.bfloat16)

    def store_masked(dst, cb, val):
        dst[0:HALO + tm, cb:cb + COL_BLOCK] = val[0:HALO + tm]
        dst[HALO + tm:rows, cb:cb + COL_BLOCK] = jnp.where(is_last, 0.0, val[HALO + tm:rows])

    def glu_block(cb):
        zv = _dot(u_s[...], win_ref[:, cb:cb + COL_BLOCK])
        zg = _dot(u_s[...], win_ref[:, d + cb:d + cb + COL_BLOCK])
        release(zv, zg)
        a = zv * jax.nn.sigmoid(zg)
        a = jnp.concatenate([a[0:HALO + tm], jnp.where(is_last, 0.0, a[HALO + tm:rows])], axis=0)
        for lt in range(0, COL_BLOCK, LANES):
            a_s[(cb + lt) // LANES, pl.ds(0, rows, stride=2), :] = a[:, lt:lt + LANES]

    def pool_proj_block(cb):
        store_masked(p_s, cb, _dot(u_s[...], win_ref[:, 2 * d + cb:2 * d + cb + COL_BLOCK]))

    row_blocks = range(0, tm, NORM_ROWS)
    conv_row_blocks = range(0, tm, CONV_ROWS)
    dot_cost = 512.0
    vector_work = []
    for rb in conv_row_blocks:
        vector_work += [(125.0, P(conv_block, rb, lt)) for lt in range(0, d, LANES)]
        vector_work += [(90.0, P(ln_block, r2)) for r2 in range(rb, rb + CONV_ROWS, NORM_ROWS)]
    pool_work = [(50.0, P(pool_block, rb, g)) for rb in conv_row_blocks for g in range(len(POOL_WINDOWS))]
    hidden = [(2 * dot_cost, P(ffn_hidden_block, cb)) for cb in range(0, d_ff, COL_BLOCK)]
    matmul_work = ([(dot_cost / 4, P(pool_map_block, g)) for g in range(len(POOL_WINDOWS))]
                   + [(2 * dot_cost, P(merge_block, cb)) for cb in col_blocks]
                   + [(dot_cost, P(out_proj_block, cb)) for cb in col_blocks]
                   + [(2 * dot_cost, P(gate_block, cb)) for cb in col_blocks]
                   + [(10.0, P(ffn_norm_block, rb)) for rb in row_blocks]
                   + hidden[:2] + [(10.0, P(norm_in))] + hidden[2:]
                   + [(d_ff / d * dot_cost, P(ffn_down_block, cb)) for cb in col_blocks])
    glu_work = ([(10.0, P(ffn_out_block, rb)) for rb in row_blocks]
                + [(2 * dot_cost * rows / tm, P(glu_block, cb)) for cb in col_blocks])
    order = (_spread(vector_work, matmul_work) + _spread(pool_work, glu_work)
             + [P(pool_proj_block, cb) for cb in col_blocks])
    names = [t.func.__name__ for t in order]

    def final(name):
        return max(i for i, n in enumerate(names) if n == name)

    assert final("merge_block") < names.index("ln_block")
    assert final("merge_block") < names.index("gate_block")
    assert final("gate_block") < names.index("norm_in")
    assert final("conv_block") < names.index("glu_block")
    assert final("pool_block") < names.index("pool_proj_block")
    for thunk in order:
        thunk()


def _resident(shape):
    zeros = (0,) * len(shape)
    return pl.BlockSpec(shape, lambda s: zeros, pipeline_mode=pl.Buffered(1))


def kernel(x, meta_tokens, g_mix, w_in, b_gate, w_dw, b_dw, ln_g, ln_b, w_conv_out, w_pool, pool_scale,
           w_pool_out, w_o, g_ffn, w_ffn_gate, w_ffn_up, w_ffn_down, g_final):
    batch, seq, d = x.shape
    assert d == D_MODEL and meta_tokens.shape == (N_META, d) and g_mix.shape[0] == 1
    assert seq % TM == 0 and N_META == HALO
    d_ff = w_ffn_gate.shape[-1]
    assert d_ff % LANES == 0
    bf16 = jnp.bfloat16
    f32 = jnp.float32

    tm = TM
    n_tiles = seq // tm
    n_total = batch * n_tiles
    halo_per_tile = tm // HALO
    halo_blocks = seq // HALO

    def new_tile(s):
        sm = jnp.minimum(s, n_total - 1)
        return sm // n_tiles, sm % n_tiles

    def xp_map(s):
        b, j = new_tile(s)
        return b, jnp.maximum(j * halo_per_tile - 1, 0), 0

    def xm_map(s):
        b, j = new_tile(s)
        return b, j, 0

    def xn_map(s):
        b, j = new_tile(s)
        return b, jnp.minimum((j + 1) * halo_per_tile, halo_blocks - 1), 0

    def old_map(s):
        so = jnp.clip(s - 2, 0, n_total - 1)
        return so // n_tiles, so % n_tiles, 0

    fused = pl.pallas_call(
        functools.partial(_fused_kernel, tm=tm, n_tiles=n_tiles, n_total=n_total,
                          seq_total=seq + N_META, d_ff=d_ff),
        grid=(n_total + 2,),
        in_specs=[
            pl.BlockSpec((None, HALO, d), xp_map),
            pl.BlockSpec((None, tm, d), xm_map),
            pl.BlockSpec((None, HALO, d), xn_map),
            pl.BlockSpec((None, tm, d), old_map),
            _resident((N_META, d)),
            _resident((1, d)),
            _resident((d, 5 * d + LANES)),
            _resident((1, 2 * d)),
            _resident((CONV_KERNEL, d)),
            _resident((1, d)),
            _resident((1, d)),
            _resident((1, d)),
            _resident((d, d + LANES)),
            _resident((len(POOL_WINDOWS), POOL_GROUP_DIM, POOL_GROUP_DIM)),
            _resident((1, d)),
            _resident((d, d + LANES)),
            _resident((d, d + LANES)),
            _resident((1, d)),
            _resident((d, d_ff)),
            _resident((d, d_ff)),
            _resident((d_ff, d + LANES)),
            _resident((1, d)),
        ],
        out_specs=pl.BlockSpec((None, tm, d), old_map),
        out_shape=jax.ShapeDtypeStruct((batch, seq, d), x.dtype),
        scratch_shapes=[
            pltpu.VMEM((tm + 2 * HALO, d), bf16),
            pltpu.VMEM((d // LANES, 2 * (tm + 2 * HALO), LANES), f32),
            pltpu.VMEM((tm + 2 * HALO, d), f32),
            pltpu.VMEM((tm, d), f32),
            pltpu.VMEM((tm, d), bf16),
            pltpu.VMEM((tm, d), bf16),
            pltpu.VMEM((tm, d), bf16),
            pltpu.VMEM((tm, d), bf16),
            pltpu.VMEM((tm, 2 * d), f32),
            pltpu.VMEM((tm, d), f32),
            pltpu.VMEM((tm, d), bf16),
            pltpu.VMEM((tm, d_ff), bf16),
            pltpu.VMEM((tm, d), f32),
        ],
        compiler_params=pltpu.CompilerParams(
            dimension_semantics=("arbitrary",), vmem_limit_bytes=VMEM_LIMIT),
        name="encoder_block",
    )

    def weight(w):
        w = w[0].astype(bf16)
        return jnp.pad(w, ((0, 0), (0, LANES))) if w.shape[1] % (SUBLANES * LANES) == 0 else w

    return fused(x, x, x, x, meta_tokens.astype(f32), g_mix.astype(f32), weight(w_in), b_gate.astype(f32),
                 w_dw[0].astype(f32), b_dw.astype(f32), ln_g.astype(f32), ln_b.astype(f32),
                 weight(w_conv_out), w_pool[0].astype(bf16), pool_scale.astype(f32),
                 weight(w_pool_out), weight(w_o),
                 g_ffn.astype(f32), weight(w_ffn_gate), weight(w_ffn_up),
                 weight(w_ffn_down), g_final.reshape(1, d).astype(f32))
```
